```python
import math
import jax, jax.numpy as jnp
from jax import lax
import numpy as np

D_MODEL = 1024
BATCH = 4
SEQ = 4096
DEPTH = 2
DEC_BATCH = 32
DEC_SEQ = 1
PAST_LEN = 16384
PAGE_SIZE = 128

H_A = 4
D_A = 64
H_B = 4
D_B = 128
MOBA_BLOCK = 256
MOBA_TOPK = 3
H_C = 4
DK_C = 128
DV_C = 128
RET_CHUNK = 128
ROPE_THETA = 500000.0
ROT_A = D_A // 4
ROT_B = D_B // 4
RET_THETA = 10000.0
N_BRANCH = 3
A_OUT = H_A * 2 * D_A
B_OUT = H_B * D_B
C_OUT = H_C * DV_C
MIX_OUT = A_OUT + B_OUT + C_OUT
SPLIT_WIDTHS = (H_A * 2 * D_A, H_A * 2 * D_A, H_A * 2 * D_A, H_B * D_B, H_B * D_B, H_B * D_B, H_C * DK_C, H_C * DK_C, H_C * DV_C, H_C * DV_C, N_BRANCH * D_MODEL)
IN_WIDTH = 3 * H_A * 2 * D_A + 3 * H_B * D_B + 2 * H_C * DK_C + 2 * H_C * DV_C + N_BRANCH * D_MODEL
D_FF = 2816
ALPHA = (2.0 * DEPTH) ** 0.25
BETA = (8.0 * DEPTH) ** -0.25
LN_EPS = 1e-5
RMS_EPS = 1e-6
ATTN_Q_BLOCK = 128
MOBA_Q_BLOCK = 32

kernel_name = 'hybrid_diffattn_moba_retnet_decoder_step'


def rope(x, pos, rot_dim, theta):
    half = rot_dim // 2
    inv = jnp.power(jnp.float32(theta), -jnp.arange(half, dtype=jnp.float32) * 2.0 / rot_dim)
    ang = pos.astype(jnp.float32)[:, None] * inv[None, :]
    shape = (pos.shape[0],) + (1,) * (x.ndim - 3) + (half,)
    cos = jnp.cos(ang).reshape(shape).astype(x.dtype)
    sin = jnp.sin(ang).reshape(shape).astype(x.dtype)
    x1 = x[..., :half]
    x2 = x[..., half:rot_dim]
    return jnp.concatenate([x1 * cos - x2 * sin, x2 * cos + x1 * sin, x[..., rot_dim:]], axis=-1)


def layer_norm(x, g, b):
    xf = x.astype(jnp.float32)
    mu = xf.mean(-1, keepdims=True)
    var = jnp.square(xf - mu).mean(-1, keepdims=True)
    return ((xf - mu) * lax.rsqrt(var + LN_EPS) * g.astype(jnp.float32) + b.astype(jnp.float32)).astype(x.dtype)


def rms_norm(x, g):
    xf = x.astype(jnp.float32)
    return (xf * lax.rsqrt(jnp.square(xf).mean(-1, keepdims=True) + RMS_EPS) * g.astype(jnp.float32)).astype(x.dtype)


def swiglu_ffn(x, w_in, w_out):
    gate, up = jnp.split(x @ w_in, 2, axis=-1)
    return (jax.nn.silu(gate) * up) @ w_out


def sweep_queries(fn, q, q_pos, blk):
    bsz, nq = q.shape[0], q.shape[1]
    if nq <= blk or nq % blk != 0:
        return fn(q, q_pos)
    n = nq // blk
    qs = jnp.moveaxis(q.reshape((bsz, n, blk) + q.shape[2:]), 1, 0)
    ps = q_pos.reshape(n, blk)
    out = lax.map(lambda qp: fn(qp[0], qp[1]), (qs, ps))
    out = jnp.moveaxis(out, 0, 1)
    return out.reshape((bsz, nq) + out.shape[3:])


def diff_attend(q, q_pos, k, v, lam):
    s = jnp.einsum('bqhcd,bkhcd->bhcqk', q, k).astype(jnp.float32) * (D_A ** -0.5)
    k_pos = jnp.arange(k.shape[1], dtype=jnp.int32)
    causal = k_pos[None, :] <= q_pos[:, None]
    s = jnp.where(causal, s, -jnp.inf)
    p = jax.nn.softmax(s, axis=-1)
    a = p[:, :, 0] - lam * p[:, :, 1]
    return jnp.einsum('bhqk,bkhe->bqhe', a.astype(v.dtype), v)


def moba_attend(q, q_pos, kblk, vblk, kmean):
    bsz, nq, nh, _ = q.shape
    n_blk = kblk.shape[2]
    qh = q.transpose(0, 2, 1, 3)
    own = q_pos // MOBA_BLOCK
    gate = jnp.einsum('bhqd,bhnd->bhqn', qh.astype(jnp.float32), kmean)
    fully_past = jnp.arange(n_blk, dtype=jnp.int32)[None, :] < own[:, None]
    gate = jnp.where(fully_past, gate, -jnp.inf)
    k_top = min(MOBA_TOPK, n_blk)
    _, top_idx = lax.top_k(gate, k_top)
    sel_ok = jnp.arange(k_top, dtype=jnp.int32)[None, :] < jnp.minimum(own, k_top)[:, None]
    bi = jnp.arange(bsz)[:, None, None, None]
    hi = jnp.arange(nh)[None, :, None, None]
    k_sel = kblk[bi, hi, top_idx]
    v_sel = vblk[bi, hi, top_idx]
    k_own = kblk[:, :, own]
    v_own = vblk[:, :, own]
    scale = D_B ** -0.5
    s_sel = jnp.einsum('bhqd,bhqnjd->bhqnj', qh, k_sel).astype(jnp.float32) * scale
    s_sel = jnp.where(sel_ok[None, None, :, :, None], s_sel, -jnp.inf).reshape(bsz, nh, nq, k_top * MOBA_BLOCK)
    own_pos = own[:, None] * MOBA_BLOCK + jnp.arange(MOBA_BLOCK, dtype=jnp.int32)[None, :]
    s_own = jnp.einsum('bhqd,bhqjd->bhqj', qh, k_own).astype(jnp.float32) * scale
    s_own = jnp.where((own_pos <= q_pos[:, None])[None, None], s_own, -jnp.inf)
    p = jax.nn.softmax(jnp.concatenate([s_sel, s_own], axis=-1), axis=-1).astype(vblk.dtype)
    p_sel = p[..., :k_top * MOBA_BLOCK].reshape(bsz, nh, nq, k_top, MOBA_BLOCK)
    p_own = p[..., k_top * MOBA_BLOCK:]
    return (jnp.einsum('bhqnj,bhqnjd->bqhd', p_sel, v_sel) + jnp.einsum('bhqj,bhqjd->bqhd', p_own, v_own))


def retention_chunk(s0, q, k, v, log_gamma):
    c = q.shape[1]
    i = jnp.arange(c, dtype=jnp.float32)
    diff = i[:, None] - i[None, :]
    decay = jnp.where(diff >= 0, jnp.exp(log_gamma[:, None, None] * jnp.maximum(diff, 0.0)), 0.0)
    inner = jnp.einsum('bihd,bjhd->bhij', q, k).astype(jnp.float32) * decay
    o_in = jnp.einsum('bhij,bjhe->bihe', inner.astype(v.dtype), v)
    q_dec = jnp.exp(log_gamma[None, :] * (i[:, None] + 1.0))
    o_cross = jnp.einsum('bihd,bhde->bihe', q, s0) * q_dec[None, :, :, None].astype(q.dtype)
    k_dec = jnp.exp(log_gamma[None, :] * (c - 1.0 - i[:, None]))
    s_new = (s0 * jnp.exp(log_gamma * c)[None, :, None, None].astype(s0.dtype) + jnp.einsum('bjhd,bjhe->bhde', k * k_dec[None, :, :, None].astype(k.dtype), v))
    return s_new, o_in + o_cross


def retention(q, k, v, s0, log_gamma):
    bsz, t = q.shape[0], q.shape[1]
    c = RET_CHUNK if t % RET_CHUNK == 0 else t
    n = t // c
    if n == 1:
        return retention_chunk(s0, q, k, v, log_gamma)
    to_chunks = lambda a: jnp.moveaxis(a.reshape((bsz, n, c) + a.shape[2:]), 1, 0)
    s_fin, out = lax.scan(lambda s, qkv: retention_chunk(s, qkv[0], qkv[1], qkv[2], log_gamma), s0, (to_chunks(q), to_chunks(k), to_chunks(v)))
    out = jnp.moveaxis(out, 0, 1)
    return s_fin, out.reshape((bsz, t) + out.shape[3:])


def block(x, pos, past_ka, past_va, past_kb, past_vb, s0, l, w_in, diff_lambda, diff_norm_g, ret_norm_g, ret_norm_b, w_branch, w_out, ln1_g, ln1_b, w_ffn_in, w_ffn_out, ln2_g, ln2_b):
    bsz, t = x.shape[0], x.shape[1]
    split_at = [int(s) for s in np.cumsum(SPLIT_WIDTHS)[:-1]]
    qa, ka, va, qb, kb, vb, qc, kc, vc, gc, gates = jnp.split(x @ w_in[l], split_at, axis=-1)

    lam_init = 0.8 - 0.6 * math.exp(-0.3 * l)
    lp = diff_lambda[l].astype(jnp.float32)
    lam = jnp.exp(jnp.sum(lp[0] * lp[1])) - jnp.exp(jnp.sum(lp[2] * lp[3])) + lam_init
    qa = rope(qa.reshape(bsz, t, H_A, 2, D_A), pos, ROT_A, ROPE_THETA)
    ka = rope(ka.reshape(bsz, t, H_A, 2, D_A), pos, ROT_A, ROPE_THETA).reshape(bsz, t, H_A, 2 * D_A)
    va = va.reshape(bsz, t, H_A, 2 * D_A)
    ka_all = jnp.concatenate([past_ka, ka], axis=1)
    va_all = jnp.concatenate([past_va, va], axis=1)
    n_keys = ka_all.shape[1]
    ka_all5 = ka_all.reshape(bsz, n_keys, H_A, 2, D_A)
    oa = sweep_queries(lambda q_, p_: diff_attend(q_, p_, ka_all5, va_all, lam), qa, pos, ATTN_Q_BLOCK)
    oa = (rms_norm(oa, diff_norm_g[l]) * (1.0 - lam_init)).reshape(bsz, t, A_OUT)

    qb = rope(qb.reshape(bsz, t, H_B, D_B), pos, ROT_B, ROPE_THETA)
    kb = rope(kb.reshape(bsz, t, H_B, D_B), pos, ROT_B, ROPE_THETA)
    vb = vb.reshape(bsz, t, H_B, D_B)
    kb_all = jnp.concatenate([past_kb, kb], axis=1)
    vb_all = jnp.concatenate([past_vb, vb], axis=1)
    n_blk = -(-n_keys // MOBA_BLOCK)
    pad = ((0, 0), (0, n_blk * MOBA_BLOCK - n_keys), (0, 0), (0, 0))
    kblk = jnp.pad(kb_all, pad).reshape(bsz, n_blk, MOBA_BLOCK, H_B, D_B).transpose(0, 3, 1, 2, 4)
    vblk = jnp.pad(vb_all, pad).reshape(bsz, n_blk, MOBA_BLOCK, H_B, D_B).transpose(0, 3, 1, 2, 4)
    kmean = kblk.astype(jnp.float32).mean(axis=3)
    ob = sweep_queries(lambda q_, p_: moba_attend(q_, p_, kblk, vblk, kmean), qb, pos, MOBA_Q_BLOCK).reshape(bsz, t, B_OUT)

    log_gamma = jnp.log1p(-jnp.power(2.0, -5.0 - jnp.arange(H_C, dtype=jnp.float32)))
    qc = rope(qc.reshape(bsz, t, H_C, DK_C), pos, DK_C, RET_THETA)
    kc = rope(kc.reshape(bsz, t, H_C, DK_C), pos, DK_C, RET_THETA) * (DK_C ** -0.5)
    vc = vc.reshape(bsz, t, H_C, DV_C)
    s_new, oc = retention(qc, kc, vc, s0, log_gamma)
    oc = layer_norm(oc, ret_norm_g[l], ret_norm_b[l]).reshape(bsz, t, C_OUT) * jax.nn.silu(gc)

    g = jax.nn.sigmoid(gates.reshape(bsz, t, N_BRANCH, D_MODEL))
    wb = w_branch[l]
    merged = (g[:, :, 0] * (oa @ wb[:A_OUT]) + g[:, :, 1] * (ob @ wb[A_OUT:A_OUT + B_OUT]) + g[:, :, 2] * (oc @ wb[A_OUT + B_OUT:]))
    x = layer_norm(ALPHA * x + merged @ w_out[l], ln1_g[l], ln1_b[l])
    x = layer_norm(ALPHA * x + swiglu_ffn(x, w_ffn_in[l], w_ffn_out[l]), ln2_g[l], ln2_b[l])
    return x, ka, va, kb, vb, s_new


def setup_inputs(seed: int = 0) -> dict:
    key = jax.random.key(seed)
    ks = jax.random.split(key, 24)
    n_pages = PAST_LEN // PAGE_SIZE
    n_phys = (5 * DEC_BATCH * n_pages) // 4

    def nrm(k, shape, scale=1.0):
        return jax.random.normal(k, shape, jnp.float32) * scale

    page_table = jax.random.permutation(ks[0], n_phys)[:DEC_BATCH * n_pages].reshape(DEC_BATCH, n_pages).astype(jnp.int32)
    return {
        'x_prompt': nrm(ks[1], (BATCH, SEQ, D_MODEL)),
        'x_sample': nrm(ks[2], (DEC_BATCH, DEC_SEQ, D_MODEL)),
        'cache_diff_k': nrm(ks[3], (DEPTH, n_phys, PAGE_SIZE, H_A, 2 * D_A)),
        'cache_diff_v': nrm(ks[4], (DEPTH, n_phys, PAGE_SIZE, H_A, 2 * D_A)),
        'cache_moba_k': nrm(ks[5], (DEPTH, n_phys, PAGE_SIZE, H_B, D_B)),
        'cache_moba_v': nrm(ks[6], (DEPTH, n_phys, PAGE_SIZE, H_B, D_B)),
        'state_ret': nrm(ks[7], (DEPTH, DEC_BATCH, H_C, DK_C, DV_C), 0.5),
        'page_table': page_table,
        'w_in': nrm(ks[8], (DEPTH, D_MODEL, IN_WIDTH), D_MODEL ** -0.5),
        'diff_lambda': nrm(ks[9], (DEPTH, 4, D_A), 0.1),
        'diff_norm_g': 1.0 + nrm(ks[10], (DEPTH, 2 * D_A), 0.02),
        'ret_norm_g': 1.0 + nrm(ks[11], (DEPTH, DV_C), 0.02),
        'ret_norm_b': nrm(ks[12], (DEPTH, DV_C), 0.02),
        'w_branch': nrm(ks[13], (DEPTH, MIX_OUT, D_MODEL), A_OUT ** -0.5),
        'w_out': nrm(ks[14], (DEPTH, D_MODEL, D_MODEL), BETA * D_MODEL ** -0.5),
        'ln1_g': 1.0 + nrm(ks[15], (DEPTH, D_MODEL), 0.02),
        'ln1_b': nrm(ks[16], (DEPTH, D_MODEL), 0.02),
        'w_ffn_in': nrm(ks[17], (DEPTH, D_MODEL, 2 * D_FF), D_MODEL ** -0.5),
        'w_ffn_out': nrm(ks[18], (DEPTH, D_FF, D_MODEL), BETA * D_FF ** -0.5),
        'ln2_g': 1.0 + nrm(ks[19], (DEPTH, D_MODEL), 0.02),
        'ln2_b': nrm(ks[20], (DEPTH, D_MODEL), 0.02),
    }


def reference(x_prompt, x_sample, cache_diff_k, cache_diff_v, cache_moba_k, cache_moba_v, state_ret, page_table, w_in, diff_lambda, diff_norm_g, ret_norm_g, ret_norm_b, w_branch, w_out, ln1_g, ln1_b, w_ffn_in, w_ffn_out, ln2_g, ln2_b):
    bp, tp = x_prompt.shape[0], x_prompt.shape[1]
    bs, ts = x_sample.shape[0], x_sample.shape[1]
    past_len = page_table.shape[1] * PAGE_SIZE
    pos_p = jnp.arange(tp, dtype=jnp.int32)
    pos_s = past_len + jnp.arange(ts, dtype=jnp.int32)
    weights = (w_in, diff_lambda, diff_norm_g, ret_norm_g, ret_norm_b, w_branch, w_out, ln1_g, ln1_b, w_ffn_in, w_ffn_out, ln2_g, ln2_b)
    hp, hs = x_prompt, x_sample
    dkp, dvp, mkp, mvp, rsp = [], [], [], [], []
    dks, dvs, mks, mvs, rss = [], [], [], [], []
    for l in range(DEPTH):
        empty_a = jnp.zeros((bp, 0, H_A, 2 * D_A), x_prompt.dtype)
        empty_b = jnp.zeros((bp, 0, H_B, D_B), x_prompt.dtype)
        s0_p = jnp.zeros((bp, H_C, DK_C, DV_C), x_prompt.dtype)
        hp, ka, va, kb, vb, sp = block(hp, pos_p, empty_a, empty_a, empty_b, empty_b, s0_p, l, *weights)
        dkp.append(ka); dvp.append(va); mkp.append(kb); mvp.append(vb); rsp.append(sp)
        past_ka = cache_diff_k[l, page_table].reshape(bs, past_len, H_A, 2 * D_A)
        past_va = cache_diff_v[l, page_table].reshape(bs, past_len, H_A, 2 * D_A)
        past_kb = cache_moba_k[l, page_table].reshape(bs, past_len, H_B, D_B)
        past_vb = cache_moba_v[l, page_table].reshape(bs, past_len, H_B, D_B)
        hs, ka_s, va_s, kb_s, vb_s, ss = block(hs, pos_s, past_ka, past_va, past_kb, past_vb, state_ret[l], l, *weights)
        dks.append(ka_s); dvs.append(va_s); mks.append(kb_s); mvs.append(vb_s); rss.append(ss)
    y_prompt = hp
    y_sample = hs
    new_diff_k_p = jnp.stack(dkp)
    new_diff_v_p = jnp.stack(dvp)
    new_moba_k_p = jnp.stack(mkp)
    new_moba_v_p = jnp.stack(mvp)
    new_ret_p = jnp.stack(rsp)
    new_diff_k_s = jnp.stack(dks)
    new_diff_v_s = jnp.stack(dvs)
    new_moba_k_s = jnp.stack(mks)
    new_moba_v_s = jnp.stack(mvs)
    new_ret_s = jnp.stack(rss)
    return (y_prompt, y_sample, new_diff_k_p, new_diff_v_p, new_moba_k_p, new_moba_v_p, new_ret_p, new_diff_k_s, new_diff_v_s, new_moba_k_s, new_moba_v_s, new_ret_s)
```

```python
import functools
import math

import jax
import jax.numpy as jnp
import numpy as np
from jax import lax
from jax.experimental import pallas as pl
from jax.experimental.pallas import tpu as pltpu

D_MODEL = 1024
DEPTH = 2
PAGE_SIZE = 128
H_A = 4
D_A = 64
H_B = 4
D_B = 128
MOBA_BLOCK = 256
MOBA_TOPK = 3
H_C = 4
DK_C = 128
DV_C = 128
RET_CHUNK = 128
ROPE_THETA = 500000.0
ROT_A = D_A // 4
ROT_B = D_B // 4
RET_THETA = 10000.0
N_BRANCH = 3
A_OUT = H_A * 2 * D_A
B_OUT = H_B * D_B
C_OUT = H_C * DV_C
SPLIT_WIDTHS = (A_OUT, A_OUT, A_OUT, B_OUT, B_OUT, B_OUT, H_C * DK_C, H_C * DK_C, C_OUT, C_OUT, N_BRANCH * D_MODEL)
D_FF = 2816
ALPHA = (2.0 * DEPTH) ** 0.25
LN_EPS = 1e-5
RMS_EPS = 1e-6

HEAD_W = 128
NEG = -1e30
VMEM_LIMIT = 48 * 1024 * 1024
ATTN_TILE = 256
DEC_PAGES = 8

BF16 = jnp.bfloat16
F32 = jnp.float32


def _params(*sem):
    return pltpu.CompilerParams(dimension_semantics=sem, vmem_limit_bytes=VMEM_LIMIT)


def _mm_kernel(x_ref, w_ref, o_ref):
    o_ref[...] = jnp.dot(x_ref[...].astype(BF16), w_ref[...], preferred_element_type=F32).astype(o_ref.dtype)


def matmul(x, w, tm=1024, tn=1024, out_dtype=F32):
    m, k = x.shape
    n = w.shape[1]
    tm = min(tm, m)
    tn = max(d for d in range(256, min(tn, n) + 1, 256) if n % d == 0)
    assert m % tm == 0
    return pl.pallas_call(
        _mm_kernel,
        grid=(m // tm, n // tn),
        in_specs=[pl.BlockSpec((tm, k), lambda i, j: (i, 0)), pl.BlockSpec((k, tn), lambda i, j: (0, j))],
        out_specs=pl.BlockSpec((tm, tn), lambda i, j: (i, j)),
        out_shape=jax.ShapeDtypeStruct((m, n), out_dtype),
        compiler_params=_params("parallel", "arbitrary"),
        name="matmul",
    )(x, w)


def _softmax_block_update(s_t, v_t, m_ref, l_ref, acc_ref):
    m_old = m_ref[...]
    m_new = jnp.maximum(m_old, jnp.max(s_t, axis=0, keepdims=True))
    alpha = jnp.exp(m_old - m_new)
    p = jnp.exp(s_t - m_new)
    l_ref[...] = alpha * l_ref[...] + jnp.sum(p, axis=0, keepdims=True)
    acc_ref[...] = alpha * acc_ref[...] + jnp.dot(v_t, p.astype(BF16), preferred_element_type=F32)
    m_ref[...] = m_new


def _init_softmax_state(m_ref, l_ref, acc_ref):
    m_ref[...] = jnp.full(m_ref.shape, NEG, F32)
    l_ref[...] = jnp.zeros(l_ref.shape, F32)
    acc_ref[...] = jnp.zeros(acc_ref.shape, F32)


def _diff_attn_kernel(lam_ref, q_ref, k_ref, vt_ref, g_ref, o_ref, m_ref, l_ref, acc_ref, *, out_scale):
    t = ATTN_TILE
    qi = pl.program_id(2)
    q = q_ref[...]
    lane = lax.broadcasted_iota(jnp.int32, q.shape, 1)
    zero = jnp.zeros_like(q)
    qp = jnp.concatenate([jnp.where(lane < D_A, q, zero), jnp.where(lane >= D_A, q, zero)], axis=0)
    _init_softmax_state(m_ref, l_ref, acc_ref)

    def scores(j):
        kj = k_ref[pl.ds(pl.multiple_of(j * t, t), t), :]
        return lax.dot_general(kj, qp, (((1,), (1,)), ((), ())), preferred_element_type=F32)

    key_idx = lax.broadcasted_iota(jnp.int32, (t, 2 * t), 0)
    qry_idx = lax.broadcasted_iota(jnp.int32, (t, 2 * t), 1)
    qry_idx = jnp.where(qry_idx >= t, qry_idx - t, qry_idx)
    _softmax_block_update(jnp.where(key_idx <= qry_idx, scores(qi), NEG), vt_ref[qi], m_ref, l_ref, acc_ref)

    def body(j, carry):
        _softmax_block_update(scores(j), vt_ref[j], m_ref, l_ref, acc_ref)
        return carry

    lax.fori_loop(0, qi, body, 0)
    o = acc_ref[...] / l_ref[...]
    o_t = o[:, :t] - lam_ref[0] * o[:, t:]
    o_t = o_t * lax.rsqrt(jnp.mean(o_t * o_t, axis=0, keepdims=True) + RMS_EPS)
    o_ref[...] = o_t.T * (g_ref[...] * out_scale)


def diff_attention_prompt(lam, q, k, vt, g, out_scale):
    b, tlen, _ = q.shape
    t = ATTN_TILE
    return pl.pallas_call(
        functools.partial(_diff_attn_kernel, out_scale=out_scale),
        grid=(b, H_A, tlen // t),
        in_specs=[
            pl.BlockSpec(memory_space=pltpu.SMEM),
            pl.BlockSpec((None, t, HEAD_W), lambda bi, h, i: (bi, i, h)),
            pl.BlockSpec((None, tlen, HEAD_W), lambda bi, h, i: (bi, 0, h)),
            pl.BlockSpec((None, None, tlen // t, HEAD_W, t), lambda bi, h, i: (bi, h, 0, 0, 0)),
            pl.BlockSpec((1, HEAD_W), lambda bi, h, i: (0, 0)),
        ],
        out_specs=pl.BlockSpec((None, t, HEAD_W), lambda bi, h, i: (bi, i, h)),
        out_shape=jax.ShapeDtypeStruct((b, tlen, H_A * HEAD_W), F32),
        scratch_shapes=[pltpu.VMEM((1, 2 * t), F32), pltpu.VMEM((1, 2 * t), F32), pltpu.VMEM((HEAD_W, 2 * t), F32)],
        compiler_params=_params("parallel", "parallel", "arbitrary"),
        name="diff_attn_prompt",
    )(lam, q, k, vt, g)


def _moba_attn_kernel(q_ref, k_ref, vt_ref, kmean_ref, o_ref, m_ref, l_ref, acc_ref, bias_ref):
    t = ATTN_TILE
    qi = pl.program_id(2)
    q = q_ref[...]
    nb = kmean_ref.shape[0]
    km = kmean_ref[...]
    km_hi = km.astype(BF16)
    km_lo = (km - km_hi.astype(F32)).astype(BF16)
    nt = (((1,), (1,)), ((), ()))
    gate = lax.dot_general(km_hi, q, nt, preferred_element_type=F32) + lax.dot_general(km_lo, q, nt, preferred_element_type=F32)
    n_idx = lax.broadcasted_iota(jnp.int32, (nb, t), 0)
    past = n_idx < qi
    gate = jnp.where(past, gate, -jnp.inf)
    rank = jnp.zeros((nb, t), jnp.int32)
    for mb in range(nb):
        gm = gate[mb:mb + 1, :]
        beats = jnp.logical_or(gm > gate, jnp.logical_and(gm == gate, n_idx > mb))
        rank = rank + jnp.where(beats, 1, 0)
    bias_ref[...] = jnp.where(jnp.logical_and(past, rank < MOBA_TOPK), 0.0, NEG)
    _init_softmax_state(m_ref, l_ref, acc_ref)

    def scores(j):
        kj = k_ref[pl.ds(pl.multiple_of(j * t, t), t), :]
        return lax.dot_general(kj, q, nt, preferred_element_type=F32)

    key_idx = lax.broadcasted_iota(jnp.int32, (t, t), 0)
    qry_idx = lax.broadcasted_iota(jnp.int32, (t, t), 1)
    _softmax_block_update(jnp.where(key_idx <= qry_idx, scores(qi), NEG), vt_ref[qi], m_ref, l_ref, acc_ref)

    def body(j, carry):
        _softmax_block_update(scores(j) + bias_ref[pl.ds(j, 1), :], vt_ref[j], m_ref, l_ref, acc_ref)
        return carry

    lax.fori_loop(0, qi, body, 0)
    o_ref[...] = (acc_ref[...] / l_ref[...]).T


def moba_attention_prompt(q, k, vt, kmean):
    b, tlen, _ = q.shape
    t = ATTN_TILE
    assert t == MOBA_BLOCK
    nb = tlen // t
    return pl.pallas_call(
        _moba_attn_kernel,
        grid=(b, H_B, tlen // t),
        in_specs=[
            pl.BlockSpec((None, t, HEAD_W), lambda bi, h, i: (bi, i, h)),
            pl.BlockSpec((None, tlen, HEAD_W), lambda bi, h, i: (bi, 0, h)),
            pl.BlockSpec((None, None, nb, HEAD_W, t), lambda bi, h, i: (bi, h, 0, 0, 0)),
            pl.BlockSpec((None, None, nb, HEAD_W), lambda bi, h, i: (bi, h, 0, 0)),
        ],
        out_specs=pl.BlockSpec((None, t, HEAD_W), lambda bi, h, i: (bi, i, h)),
        out_shape=jax.ShapeDtypeStruct((b, tlen, H_B * HEAD_W), F32),
        scratch_shapes=[pltpu.VMEM((1, t), F32), pltpu.VMEM((1, t), F32), pltpu.VMEM((HEAD_W, t), F32), pltpu.VMEM((nb, t), F32)],
        compiler_params=_params("parallel", "parallel", "arbitrary"),
        name="moba_attn_prompt",
    )(q, k, vt, kmean)


def _block_mean_kernel(k_ref, o_ref):
    nb = o_ref.shape[0]
    k = k_ref[...].reshape(nb, MOBA_BLOCK, HEAD_W)
    o_ref[...] = jnp.sum(k, axis=1) * (1.0 / MOBA_BLOCK)


def moba_block_means(k):
    b, tlen, _ = k.shape
    nb = tlen // MOBA_BLOCK
    return pl.pallas_call(
        _block_mean_kernel,
        grid=(b, H_B),
        in_specs=[pl.BlockSpec((None, tlen, HEAD_W), lambda bi, h: (bi, 0, h))],
        out_specs=pl.BlockSpec((None, None, nb, HEAD_W), lambda bi, h: (bi, h, 0, 0)),
        out_shape=jax.ShapeDtypeStruct((b, H_B, nb, HEAD_W), F32),
        compiler_params=_params("parallel", "parallel"),
        name="moba_block_means",
    )(k)


def _ret_prompt_kernel(q_ref, k_ref, v_ref, gc_ref, decay_ref, qdec_ref, kdec_ref, gpow_ref, lng_ref, lnb_ref, o_ref, s_ref, s_scr):
    c = RET_CHUNK
    n_chunks = q_ref.shape[0] // c
    s_scr[...] = jnp.zeros(s_scr.shape, F32)
    decay = decay_ref[...]
    qdec = qdec_ref[...]
    kdec = kdec_ref[...]
    gpow = gpow_ref[0:1, :]
    lng = lng_ref[...]
    lnb = lnb_ref[...]

    def body(ci, carry):
        rows = pl.ds(pl.multiple_of(ci * c, c), c)
        q = q_ref[rows, :]
        k = k_ref[rows, :]
        vb = v_ref[rows, :].astype(BF16)
        qb = q.astype(BF16)
        inner = lax.dot_general(qb, k.astype(BF16), (((1,), (1,)), ((), ())), preferred_element_type=F32) * decay
        o = jnp.dot(inner.astype(BF16), vb, preferred_element_type=F32)
        s = s_scr[...]
        o = o + jnp.dot(qb, s.astype(BF16), preferred_element_type=F32) * qdec
        kd_t = (k * kdec).T.astype(BF16)
        s_scr[...] = s * gpow + jnp.dot(kd_t, vb, preferred_element_type=F32)
        mu = jnp.mean(o, axis=-1, keepdims=True)
        d = o - mu
        y = d * lax.rsqrt(jnp.mean(d * d, axis=-1, keepdims=True) + LN_EPS) * lng + lnb
        gc = gc_ref[rows, :]
        o_ref[rows, :] = y * (gc * jax.nn.sigmoid(gc))
        return carry

    lax.fori_loop(0, n_chunks, body, 0)
    s_ref[...] = s_scr[...]


def _retention_tables():
    log_gamma = jnp.log1p(-jnp.power(2.0, -5.0 - jnp.arange(H_C, dtype=F32)))
    c = RET_CHUNK
    i = jnp.arange(c, dtype=F32)
    diff = i[:, None] - i[None, :]
    decay = jnp.where(diff >= 0, jnp.exp(log_gamma[:, None, None] * jnp.maximum(diff, 0.0)), 0.0)
    qdec = jnp.exp(log_gamma[:, None] * (i[None, :] + 1.0))
    kdec = jnp.exp(log_gamma[:, None] * (c - 1.0 - i[None, :]))
    gpow = jnp.exp(log_gamma * c)
    bc = lambda a: jnp.broadcast_to(a[:, :, None], (H_C, c, HEAD_W))
    return decay, bc(qdec), bc(kdec), jnp.broadcast_to(gpow[:, None, None], (H_C, 8, HEAD_W)), log_gamma


def retention_prompt(q, k, v, gc, ln_g, ln_b):
    b, tlen, _ = q.shape
    c = RET_CHUNK
    decay, qdec, kdec, gpow, _ = _retention_tables()
    seq = pl.BlockSpec((None, tlen, HEAD_W), lambda bi, h: (bi, 0, h))
    tab = pl.BlockSpec((None, c, HEAD_W), lambda bi, h: (h, 0, 0))
    vec = pl.BlockSpec((1, HEAD_W), lambda bi, h: (0, 0))
    return pl.pallas_call(
        _ret_prompt_kernel,
        grid=(b, H_C),
        in_specs=[seq, seq, seq, seq, tab, tab, tab, pl.BlockSpec((None, 8, HEAD_W), lambda bi, h: (h, 0, 0)), vec, vec],
        out_specs=[seq, pl.BlockSpec((None, None, DK_C, DV_C), lambda bi, h: (bi, h, 0, 0))],
        out_shape=[jax.ShapeDtypeStruct((b, tlen, C_OUT), F32), jax.ShapeDtypeStruct((b, H_C, DK_C, DV_C), F32)],
        scratch_shapes=[pltpu.VMEM((DK_C, DV_C), F32)],
        compiler_params=_params("parallel", "parallel"),
        name="retention_prompt",
    )(q, k, v, gc, decay, qdec, kdec, gpow, ln_g, ln_b)


def _row_to_col(row, n):
    return jnp.broadcast_to(row, (n, n)).T


def _diff_decode_kernel(pt_ref, qbd_ref, knew_ref, vnew_ref, *rest):
    g = DEC_PAGES
    k_refs, v_refs = rest[:g], rest[g:2 * g]
    o_ref, m_ref, l_ref, acc_ref = rest[2 * g:]
    step = pl.program_id(1)
    qbd = qbd_ref[...]
    ncol = qbd.shape[1]
    width = acc_ref.shape[1]

    @pl.when(step == 0)
    def _():
        s_self = jnp.dot(knew_ref[...].astype(BF16), qbd, preferred_element_type=F32)
        m_ref[...] = s_self[0:1, :]
        l_ref[...] = jnp.ones(l_ref.shape, F32)
        acc_ref[...] = jnp.broadcast_to(vnew_ref[0:1, :], acc_ref.shape)

    kcat = jnp.concatenate([r[...].astype(BF16) for r in k_refs], axis=0)
    vcat = jnp.concatenate([r[...].astype(BF16) for r in v_refs], axis=0)
    s_t = jnp.dot(kcat, qbd, preferred_element_type=F32)
    m_old = m_ref[...]
    m_new = jnp.maximum(m_old, jnp.max(s_t, axis=0, keepdims=True))
    alpha = jnp.exp(m_old - m_new)
    p = jnp.exp(s_t - m_new)
    l_ref[...] = alpha * l_ref[...] + jnp.sum(p, axis=0, keepdims=True)
    m_ref[...] = m_new
    alpha_col = jnp.tile(_row_to_col(alpha, ncol), (1, width // ncol))
    acc_ref[...] = alpha_col * acc_ref[...] + jnp.dot(p.T.astype(BF16), vcat, preferred_element_type=F32)

    @pl.when(step == pl.num_programs(1) - 1)
    def _():
        inv_col = jnp.tile(_row_to_col(1.0 / l_ref[...], ncol), (1, width // ncol))
        o_ref[...] = acc_ref[...] * inv_col


def diff_attention_decode(layer, page_table, qbd, knew, vnew, cache_k, cache_v):
    b, n_pages = page_table.shape
    g = DEC_PAGES
    width = cache_k.shape[-1]
    ncol = qbd.shape[-1]

    def page_spec(i):
        return pl.BlockSpec((None, None, PAGE_SIZE, width), lambda bi, s, pt: (layer, pt[bi, s * g + i], 0, 0))

    row8 = pl.BlockSpec((None, 8, width), lambda bi, s, pt: (bi, 0, 0))
    grid_spec = pltpu.PrefetchScalarGridSpec(
        num_scalar_prefetch=1,
        grid=(b, n_pages // g),
        in_specs=[pl.BlockSpec((None, width, ncol), lambda bi, s, pt: (bi, 0, 0)), row8, row8]
        + [page_spec(i) for i in range(g)] + [page_spec(i) for i in range(g)],
        out_specs=pl.BlockSpec((None, ncol, width), lambda bi, s, pt: (bi, 0, 0)),
        scratch_shapes=[pltpu.VMEM((1, ncol), F32), pltpu.VMEM((1, ncol), F32), pltpu.VMEM((ncol, width), F32)],
    )
    return pl.pallas_call(
        _diff_decode_kernel,
        grid_spec=grid_spec,
        out_shape=jax.ShapeDtypeStruct((b, ncol, width), F32),
        compiler_params=_params("parallel", "arbitrary"),
        name="diff_attn_decode",
    )(page_table, qbd, knew, vnew, *([cache_k] * g), *([cache_v] * g))


PAGES_PER_BLOCK = MOBA_BLOCK // PAGE_SIZE


def _moba_block_sum_kernel(pt_ref, *rest):
    g = DEC_PAGES
    k_refs, o_ref = rest[:g], rest[g]
    for blk in range(g // PAGES_PER_BLOCK):
        tot = jnp.sum(k_refs[blk * PAGES_PER_BLOCK][...], axis=0, keepdims=True)
        for u in range(1, PAGES_PER_BLOCK):
            tot = tot + jnp.sum(k_refs[blk * PAGES_PER_BLOCK + u][...], axis=0, keepdims=True)
        o_ref[blk:blk + 1, :] = tot


def moba_block_sums_decode(layer, page_table, cache_k):
    b, n_pages = page_table.shape
    g = DEC_PAGES
    width = cache_k.shape[-1]
    nblk_step = g // PAGES_PER_BLOCK
    n_blocks = n_pages // PAGES_PER_BLOCK

    def page_spec(i):
        return pl.BlockSpec((None, None, PAGE_SIZE, width), lambda bi, s, pt: (layer, pt[bi, s * g + i], 0, 0))

    grid_spec = pltpu.PrefetchScalarGridSpec(
        num_scalar_prefetch=1,
        grid=(b, n_pages // g),
        in_specs=[page_spec(i) for i in range(g)],
        out_specs=pl.BlockSpec((None, None, nblk_step, width), lambda bi, s, pt: (bi, s, 0, 0)),
    )
    out = pl.pallas_call(
        _moba_block_sum_kernel,
        grid_spec=grid_spec,
        out_shape=jax.ShapeDtypeStruct((b, n_pages // g, nblk_step, width), F32),
        compiler_params=_params("parallel", "parallel"),
        name="moba_block_sums_decode",
    )(page_table, *([cache_k] * g))
    return out.reshape(b, n_blocks, width)


def _moba_topk_kernel(ksum_ref, q_ref, o_ref):
    nb = ksum_ref.shape[0]
    prod = (ksum_ref[...] * (1.0 / MOBA_BLOCK)) * q_ref[0:1, :]
    n_idx = lax.broadcasted_iota(jnp.int32, (nb, 1), 0).astype(F32)
    row = lax.broadcasted_iota(jnp.int32, o_ref.shape, 0)
    col = lax.broadcasted_iota(jnp.int32, o_ref.shape, 1)
    out = jnp.zeros(o_ref.shape, F32)
    for h in range(H_B):
        gate = jnp.sum(prod[:, h * HEAD_W:(h + 1) * HEAD_W], axis=-1, keepdims=True)
        for t in range(MOBA_TOPK):
            best = jnp.max(gate, axis=0, keepdims=True)
            idx = jnp.min(jnp.where(gate == best, n_idx, float(nb)), axis=0, keepdims=True)
            gate = jnp.where(n_idx == idx, -jnp.inf, gate)
            out = out + jnp.where(jnp.logical_and(row == h, col == t), idx, 0.0)
    o_ref[...] = out.astype(jnp.int32)


def moba_topk_decode(ksum, q8):
    b, nb, width = ksum.shape
    return pl.pallas_call(
        _moba_topk_kernel,
        grid=(b,),
        in_specs=[pl.BlockSpec((None, nb, width), lambda bi: (bi, 0, 0)), pl.BlockSpec((None, 8, width), lambda bi: (bi, 0, 0))],
        out_specs=pl.BlockSpec((None, 8, 128), lambda bi: (bi, 0, 0)),
        out_shape=jax.ShapeDtypeStruct((b, 8, 128), jnp.int32),
        compiler_params=_params("parallel"),
        name="moba_topk_decode",
    )(ksum, q8)


def _moba_decode_attn_kernel(pt_ref, sel_ref, q_ref, knew_ref, vnew_ref, *rest):
    n = MOBA_TOPK * PAGES_PER_BLOCK
    k_refs, v_refs, o_ref = rest[:n], rest[n:2 * n], rest[2 * n]
    q = q_ref[...]
    kcat = jnp.concatenate([r[...].astype(BF16) for r in k_refs], axis=0)
    vcat = jnp.concatenate([r[...].astype(BF16) for r in v_refs], axis=0)
    s = lax.dot_general(q.astype(BF16), kcat, (((1,), (1,)), ((), ())), preferred_element_type=F32)
    s_self = jnp.sum(q * knew_ref[...], axis=-1, keepdims=True)
    m = jnp.maximum(jnp.max(s, axis=-1, keepdims=True), s_self)
    p = jnp.exp(s - m)
    p_self = jnp.exp(s_self - m)
    denom = jnp.sum(p, axis=-1, keepdims=True) + p_self
    o_ref[...] = (jnp.dot(p.astype(BF16), vcat, preferred_element_type=F32) + p_self * vnew_ref[...]) / denom


def moba_attention_decode(layer, page_table, sel, q8, knew, vnew, cache_k, cache_v):
    b = page_table.shape[0]

    def page_spec(t, u):
        def index(bi, h, pt, sl):
            return (layer, pt[bi, sl[bi, h * MOBA_TOPK + t] * PAGES_PER_BLOCK + u], 0, h)
        return pl.BlockSpec((None, None, PAGE_SIZE, HEAD_W), index)

    pages = [page_spec(t, u) for t in range(MOBA_TOPK) for u in range(PAGES_PER_BLOCK)]
    row8 = pl.BlockSpec((None, 8, HEAD_W), lambda bi, h, pt, sl: (bi, 0, h))
    grid_spec = pltpu.PrefetchScalarGridSpec(
        num_scalar_prefetch=2,
        grid=(b, H_B),
        in_specs=[row8, row8, row8] + pages + pages,
        out_specs=row8,
    )
    n = len(pages)
    return pl.pallas_call(
        _moba_decode_attn_kernel,
        grid_spec=grid_spec,
        out_shape=jax.ShapeDtypeStruct((b, 8, H_B * HEAD_W), F32),
        compiler_params=_params("parallel", "parallel"),
        name="moba_attn_decode",
    )(page_table, sel, q8, knew, vnew, *([cache_k] * n), *([cache_v] * n))


def _ret_decode_kernel(s0_ref, qcol_ref, kcol_ref, qrow_ref, krow_ref, vrow_ref, o_ref, s_ref, *, gammas):
    for h in range(H_C):
        s0 = s0_ref[h]
        v = vrow_ref[h]
        qk = jnp.sum(qrow_ref[h] * krow_ref[h], axis=-1, keepdims=True)
        cross = jnp.sum(qcol_ref[h] * s0, axis=0, keepdims=True) * gammas[h]
        o_ref[h] = qk * v + cross
        s_ref[h] = s0 * gammas[h] + kcol_ref[h] * v


def retention_decode(s0, q, k, v):
    b = s0.shape[0]
    gammas = tuple(float(np.exp(np.log1p(-np.float32(2.0) ** np.float32(-5.0 - h)))) for h in range(H_C))
    col = pl.BlockSpec((None, H_C, DK_C, 1), lambda bi: (bi, 0, 0, 0))
    row = pl.BlockSpec((None, H_C, 1, DK_C), lambda bi: (bi, 0, 0, 0))
    st = pl.BlockSpec((None, H_C, DK_C, DV_C), lambda bi: (bi, 0, 0, 0))
    return pl.pallas_call(
        functools.partial(_ret_decode_kernel, gammas=gammas),
        grid=(b,),
        in_specs=[st, col, col, row, row, row],
        out_specs=[row, st],
        out_shape=[jax.ShapeDtypeStruct((b, H_C, 1, DV_C), F32), jax.ShapeDtypeStruct((b, H_C, DK_C, DV_C), F32)],
        compiler_params=_params("parallel"),
        name="retention_decode",
    )(s0, q[..., None], k[..., None], q[:, :, None, :], k[:, :, None, :], v[:, :, None, :])


def _rope(x, pos, rot_dim, theta):
    half = rot_dim // 2
    inv = jnp.power(jnp.float32(theta), -jnp.arange(half, dtype=F32) * 2.0 / rot_dim)
    ang = pos.astype(F32)[:, None] * inv[None, :]
    shape = (pos.shape[0],) + (1,) * (x.ndim - 3) + (half,)
    cos = jnp.cos(ang).reshape(shape)
    sin = jnp.sin(ang).reshape(shape)
    x1 = x[..., :half]
    x2 = x[..., half:rot_dim]
    return jnp.concatenate([x1 * cos - x2 * sin, x2 * cos + x1 * sin, x[..., rot_dim:]], axis=-1)


def _layer_norm(x, g, b):
    mu = x.mean(-1, keepdims=True)
    var = jnp.square(x - mu).mean(-1, keepdims=True)
    return (x - mu) * lax.rsqrt(var + LN_EPS) * g + b


def _rms_norm(x, g):
    return x * lax.rsqrt(jnp.square(x).mean(-1, keepdims=True) + RMS_EPS) * g


def _to_blocked_t(v, bsz, tlen, heads):
    t = ATTN_TILE
    return v.reshape(bsz, tlen // t, t, heads, HEAD_W).transpose(0, 3, 1, 4, 2)


def _pad_rows8(x):
    return jnp.pad(x[:, None, :], ((0, 0), (0, 7), (0, 0)))


def _layer(l, x, pos, is_prompt, past, w):
    bsz, t = x.shape[0], x.shape[1]
    m = bsz * t
    split_at = [int(s) for s in np.cumsum(SPLIT_WIDTHS)[:-1]]
    z = matmul(x.reshape(m, D_MODEL), w["w_in"][l]).reshape(bsz, t, -1)
    qa, ka, va, qb, kb, vb, qc, kc, vc, gc, gates = jnp.split(z, split_at, axis=-1)

    lam_init = 0.8 - 0.6 * math.exp(-0.3 * l)
    lp = w["diff_lambda"][l]
    lam = jnp.exp(jnp.sum(lp[0] * lp[1])) - jnp.exp(jnp.sum(lp[2] * lp[3])) + lam_init
    qa = _rope(qa.reshape(bsz, t, H_A, 2, D_A), pos, ROT_A, ROPE_THETA).reshape(bsz, t, A_OUT)
    ka = _rope(ka.reshape(bsz, t, H_A, 2, D_A), pos, ROT_A, ROPE_THETA).reshape(bsz, t, A_OUT)
    qb = _rope(qb.reshape(bsz, t, H_B, D_B), pos, ROT_B, ROPE_THETA).reshape(bsz, t, B_OUT)
    kb = _rope(kb.reshape(bsz, t, H_B, D_B), pos, ROT_B, ROPE_THETA).reshape(bsz, t, B_OUT)
    qc = _rope(qc.reshape(bsz, t, H_C, DK_C), pos, DK_C, RET_THETA).reshape(bsz, t, H_C * DK_C)
    kc = (_rope(kc.reshape(bsz, t, H_C, DK_C), pos, DK_C, RET_THETA) * (DK_C ** -0.5)).reshape(bsz, t, H_C * DK_C)
    qa_s = qa * (D_A ** -0.5)
    qb_s = qb * (D_B ** -0.5)
    norm_scale = 1.0 - lam_init

    if is_prompt:
        oa = diff_attention_prompt(lam.reshape(1), qa_s.astype(BF16), ka.astype(BF16), _to_blocked_t(va.astype(BF16), bsz, t, H_A),
                                   w["diff_norm_g"][l][None, :], norm_scale)
        kmean = moba_block_means(kb)
        ob = moba_attention_prompt(qb_s.astype(BF16), kb.astype(BF16), _to_blocked_t(vb.astype(BF16), bsz, t, H_B), kmean)
        oc, s_new = retention_prompt(qc, kc, vc, gc, w["ret_norm_g"][l][None, :], w["ret_norm_b"][l][None, :])
    else:
        page_table, cache_dk, cache_dv, cache_mk, cache_mv, s0 = past
        q4 = qa_s.reshape(bsz, H_A, 2, D_A)
        eye_h = jnp.eye(H_A, dtype=F32)
        eye_c = jnp.eye(2, dtype=F32)
        qbd = jnp.einsum("bhcd,hg,ce->bhcdge", q4, eye_h, eye_c).reshape(bsz, A_OUT, 2 * H_A)
        qbd = jnp.pad(qbd, ((0, 0), (0, 0), (0, HEAD_W - 2 * H_A))).astype(BF16)
        o_all = diff_attention_decode(l, page_table, qbd, _pad_rows8(ka[:, 0]), _pad_rows8(va[:, 0]), cache_dk, cache_dv)
        o_all = o_all[:, :2 * H_A].reshape(bsz, H_A, 2, H_A, HEAD_W)
        o_maps = jnp.einsum("bhcge,hg->bhce", o_all, eye_h)
        oa = o_maps[:, :, 0] - lam * o_maps[:, :, 1]
        oa = (_rms_norm(oa, w["diff_norm_g"][l]) * norm_scale).reshape(bsz, 1, A_OUT)

        qb8, kb8, vb8 = _pad_rows8(qb_s[:, 0]), _pad_rows8(kb[:, 0]), _pad_rows8(vb[:, 0])
        ksum = moba_block_sums_decode(l, page_table, cache_mk)
        sel = moba_topk_decode(ksum, qb8)[:, :H_B, :MOBA_TOPK].reshape(bsz, H_B * MOBA_TOPK)
        ob = moba_attention_decode(l, page_table, sel, qb8, kb8, vb8, cache_mk, cache_mv)[:, 0:1, :]

        o_ret, s_new = retention_decode(s0, qc.reshape(bsz, H_C, DK_C), kc.reshape(bsz, H_C, DK_C), vc.reshape(bsz, H_C, DV_C))
        oc = _layer_norm(o_ret.reshape(bsz, 1, H_C, DV_C), w["ret_norm_g"][l], w["ret_norm_b"][l]).reshape(bsz, 1, C_OUT)
        oc = oc * jax.nn.silu(gc)

    g = jax.nn.sigmoid(gates.reshape(bsz, t, N_BRANCH, D_MODEL))
    wb = w["w_branch"][l]
    pa = matmul(oa.reshape(m, A_OUT), wb[:A_OUT]).reshape(bsz, t, D_MODEL)
    pb = matmul(ob.reshape(m, B_OUT), wb[A_OUT:A_OUT + B_OUT]).reshape(bsz, t, D_MODEL)
    pc = matmul(oc.reshape(m, C_OUT), wb[A_OUT + B_OUT:]).reshape(bsz, t, D_MODEL)
    merged = g[:, :, 0] * pa + g[:, :, 1] * pb + g[:, :, 2] * pc
    x = _layer_norm(ALPHA * x + matmul(merged.reshape(m, D_MODEL), w["w_out"][l]).reshape(bsz, t, D_MODEL), w["ln1_g"][l], w["ln1_b"][l])
    hcat = matmul(x.reshape(m, D_MODEL), w["w_ffn_in"][l])
    hmid = jax.nn.silu(hcat[:, :D_FF]) * hcat[:, D_FF:]
    ffn = matmul(hmid, w["w_ffn_out"][l], tm=512).reshape(bsz, t, D_MODEL)
    x = _layer_norm(ALPHA * x + ffn, w["ln2_g"][l], w["ln2_b"][l])
    kv = (ka.reshape(bsz, t, H_A, 2 * D_A), va.reshape(bsz, t, H_A, 2 * D_A), kb.reshape(bsz, t, H_B, D_B), vb.reshape(bsz, t, H_B, D_B))
    return x, kv, s_new


def kernel(x_prompt, x_sample, cache_diff_k, cache_diff_v, cache_moba_k, cache_moba_v, state_ret, page_table, w_in, diff_lambda, diff_norm_g, ret_norm_g, ret_norm_b, w_branch, w_out, ln1_g, ln1_b, w_ffn_in, w_ffn_out, ln2_g, ln2_b):
    tp = x_prompt.shape[1]
    ts = x_sample.shape[1]
    assert ts == 1 and tp % ATTN_TILE == 0
    n_pages = page_table.shape[1]
    assert n_pages % DEC_PAGES == 0 and DEC_PAGES % PAGES_PER_BLOCK == 0
    past_len = n_pages * PAGE_SIZE
    pos_p = jnp.arange(tp, dtype=jnp.int32)
    pos_s = past_len + jnp.arange(ts, dtype=jnp.int32)
    w = dict(w_in=w_in.astype(BF16), diff_lambda=diff_lambda, diff_norm_g=diff_norm_g, ret_norm_g=ret_norm_g, ret_norm_b=ret_norm_b,
             w_branch=w_branch.astype(BF16), w_out=w_out.astype(BF16), ln1_g=ln1_g, ln1_b=ln1_b,
             w_ffn_in=w_ffn_in.astype(BF16), w_ffn_out=w_ffn_out.astype(BF16), ln2_g=ln2_g, ln2_b=ln2_b)
    flat = lambda c: c.reshape(c.shape[0], c.shape[1], PAGE_SIZE, -1)
    cdk, cdv, cmk, cmv = flat(cache_diff_k), flat(cache_diff_v), flat(cache_moba_k), flat(cache_moba_v)
    hp, hs = x_prompt, x_sample
    outs_p, outs_s = [], []
    for l in range(DEPTH):
        hp, kv_p, sp = _layer(l, hp, pos_p, True, None, w)
        outs_p.append(kv_p + (sp,))
        hs, kv_s, ss = _layer(l, hs, pos_s, False, (page_table, cdk, cdv, cmk, cmv, state_ret[l]), w)
        outs_s.append(kv_s + (ss,))
    stack = lambda outs, i: jnp.stack([o[i] for o in outs])
    return (hp, hs) + tuple(stack(outs_p, i) for i in range(5)) + tuple(stack(outs_s, i) for i in range(5))
```
